```python
import math
import jax, jax.numpy as jnp
from jax import lax
import numpy as np

D_MODEL = 1024
BATCH = 4
SEQ = 4096
DEPTH = 1
DEC_BATCH = 128
DEC_SEQ = 8
PAST_LEN = 2048
PAGE_SIZE = 128

HEAD_DIM = 64
HEADS_PER_GROUP = 4
ATTN_GROUPS = ((128, 1), (512, 4), (2048, 16))
N_GROUPS = len(ATTN_GROUPS)
GROUP_WIDTH = HEADS_PER_GROUP * HEAD_DIM
WIN_STEPS = 128
BLOCK = WIN_STEPS
D_CONV = D_MODEL // 2
CONV_W = 3
D_FF = 4 * D_MODEL
ROPE_THETA = 10000.0
ALPHA = (2.0 * DEPTH) ** 0.25
BETA = (8.0 * DEPTH) ** -0.25
LN_EPS = 1e-5
ATTN_SCALE = HEAD_DIM ** -0.5
N_IN = 3 * D_CONV + 3 * N_GROUPS * GROUP_WIDTH + 2 * D_MODEL

kernel_name = 'hybrid_shortconv_dilated_swa_decoder_step'


def layer_norm(x, g, b):
    xf = x.astype(jnp.float32)
    mu = jnp.mean(xf, axis=-1, keepdims=True)
    var = jnp.mean(jnp.square(xf - mu), axis=-1, keepdims=True)
    return ((xf - mu) * lax.rsqrt(var + LN_EPS) * g.astype(jnp.float32) + b.astype(jnp.float32)).astype(x.dtype)


def rope(x, pos):
    inv = 1.0 / (ROPE_THETA ** (jnp.arange(0, HEAD_DIM, 2, dtype=jnp.float32) / HEAD_DIM))
    ang = pos.astype(jnp.float32)[:, None] * inv[None, :]
    cos = jnp.cos(ang)[None, :, None, :]
    sin = jnp.sin(ang)[None, :, None, :]
    xf = x.astype(jnp.float32)
    x1, x2 = xf[..., :HEAD_DIM // 2], xf[..., HEAD_DIM // 2:]
    return jnp.concatenate([x1 * cos - x2 * sin, x2 * cos + x1 * sin], axis=-1).astype(x.dtype)


def masked_softmax_lse(s, valid):
    s = jnp.where(valid, s, -jnp.inf)
    m = jnp.max(s, axis=-1, keepdims=True)
    p = jnp.exp(s - m)
    denom = jnp.sum(p, axis=-1, keepdims=True)
    return p / denom, (m + jnp.log(denom))[..., 0]


def dilated_attn_prompt(q, k, v, dil):
    n, s_len, h, e = q.shape
    L = s_len // dil
    nb = -(-L // BLOCK)
    lp = nb * BLOCK

    def to_blocks(t):
        t = t.reshape(n, L, dil, h, e).transpose(0, 2, 1, 3, 4)
        t = jnp.pad(t, ((0, 0), (0, 0), (0, lp - L), (0, 0), (0, 0)))
        return t.reshape(n, dil, nb, BLOCK, h, e)

    def with_prev(t):
        prev = jnp.pad(t[:, :, :-1], ((0, 0), (0, 0), (1, 0), (0, 0), (0, 0), (0, 0)))
        return jnp.concatenate([prev, t], axis=3)

    qb = to_blocks(q)
    kk = with_prev(to_blocks(k))
    vv = with_prev(to_blocks(v))
    s = jnp.einsum('nrbqhe,nrbkhe->nrbhqk', qb, kk, preferred_element_type=jnp.float32) * ATTN_SCALE
    qi = jnp.arange(nb)[:, None] * BLOCK + jnp.arange(BLOCK)[None, :]
    ki = (jnp.arange(nb)[:, None] - 1) * BLOCK + jnp.arange(2 * BLOCK)[None, :]
    dist = qi[:, :, None] - ki[:, None, :]
    valid = (dist >= 0) & (dist <= WIN_STEPS) & (ki[:, None, :] >= 0)
    p, lse = masked_softmax_lse(s, valid[:, None])
    o = jnp.einsum('nrbhqk,nrbkhe->nrbqhe', p.astype(v.dtype), vv)
    o = o.reshape(n, dil, lp, h, e)[:, :, :L].transpose(0, 2, 1, 3, 4).reshape(n, s_len, h, e)
    lse = lse.transpose(0, 1, 2, 4, 3).reshape(n, dil, lp, h)[:, :, :L].transpose(0, 2, 1, 3).reshape(n, s_len, h)
    return o, lse


def dilated_attn_sample(q, k, v, kv_prev, dil, window):
    n, t_len, h, e = q.shape
    w = kv_prev.shape[1]
    kv_all = jnp.concatenate([kv_prev.astype(k.dtype), jnp.stack([k, v], axis=2)], axis=1)
    idx = w + jnp.arange(t_len)[:, None] - jnp.arange(WIN_STEPS + 1)[None, :] * dil
    valid = idx >= 0
    g = kv_all[:, jnp.maximum(idx, 0)]
    s = jnp.einsum('nthe,ntjhe->nthj', q, g[:, :, :, 0], preferred_element_type=jnp.float32) * ATTN_SCALE
    p, lse = masked_softmax_lse(s, valid[None, :, None, :])
    o = jnp.einsum('nthj,ntjhe->nthe', p.astype(v.dtype), g[:, :, :, 1])
    new_kv = kv_all[:, -min(window, w + t_len):]
    return o, lse, new_kv


def short_conv(gb, gc, hc, conv_prev, conv_w):
    u = gc * hc
    u_pad = jnp.concatenate([conv_prev.astype(u.dtype), u], axis=1)
    t_len = u.shape[1]
    y = conv_w[0] * u_pad[:, 0:t_len]
    for j in range(1, CONV_W):
        y = y + conv_w[j] * u_pad[:, j:j + t_len]
    return gb * y, u_pad[:, -(CONV_W - 1):]


def decoder_layer(x, pos, conv_prev, kv_prev, w_in, b_gate, conv_w, w_conv_out, w_attn_out, w_o,
                  ln1_g, ln1_b, w_up, w_down, ln2_g, ln2_b):
    n, t_len, _ = x.shape
    z = jnp.einsum('btd,dn->btn', x, w_in)
    gb = z[..., 0:D_CONV]
    gc = z[..., D_CONV:2 * D_CONV]
    hc = z[..., 2 * D_CONV:3 * D_CONV]
    yc, conv_state = short_conv(gb, gc, hc, conv_prev, conv_w)

    outs, lses, kv_new = [], [], []
    for gi, (window, dil) in enumerate(ATTN_GROUPS):
        off = 3 * D_CONV + gi * 3 * GROUP_WIDTH
        q = rope(z[..., off:off + GROUP_WIDTH].reshape(n, t_len, HEADS_PER_GROUP, HEAD_DIM), pos)
        k = rope(z[..., off + GROUP_WIDTH:off + 2 * GROUP_WIDTH].reshape(n, t_len, HEADS_PER_GROUP, HEAD_DIM), pos)
        v = z[..., off + 2 * GROUP_WIDTH:off + 3 * GROUP_WIDTH].reshape(n, t_len, HEADS_PER_GROUP, HEAD_DIM)
        if kv_prev is None:
            o, lse = dilated_attn_prompt(q, k, v, dil)
            kv = jnp.stack([k, v], axis=2)[:, -min(window, t_len):]
        else:
            o, lse, kv = dilated_attn_sample(q, k, v, kv_prev[gi], dil, window)
        outs.append(o)
        lses.append(lse)
        kv_new.append(kv)

    wts = jax.nn.softmax(jnp.stack(lses, axis=0), axis=0)
    ya = wts[0][..., None] * outs[0]
    for gi in range(1, N_GROUPS):
        ya = ya + wts[gi][..., None] * outs[gi]
    ya = ya.astype(x.dtype).reshape(n, t_len, GROUP_WIDTH)

    gates = jax.nn.sigmoid(z[..., -2 * D_MODEL:] + b_gate)
    mixed = gates[..., :D_MODEL] * (yc @ w_conv_out) + gates[..., D_MODEL:] * (ya @ w_attn_out)
    x1 = layer_norm(ALPHA * x + mixed @ w_o, ln1_g, ln1_b)
    hid = jnp.square(jax.nn.relu(x1 @ w_up))
    x2 = layer_norm(ALPHA * x1 + hid @ w_down, ln2_g, ln2_b)
    return x2, conv_state, kv_new


def setup_inputs(seed: int = 0) -> dict:
    key = jax.random.key(seed)
    ks = jax.random.split(key, 20)
    f32 = jnp.float32

    def nrm(k, shape, scale):
        return jax.random.normal(k, shape, f32) * scale

    def kv_cache(k, window):
        return nrm(k, (DEPTH, DEC_BATCH, min(window, PAST_LEN), 2, HEADS_PER_GROUP, HEAD_DIM), 1.0)

    col_scale = jnp.concatenate(
        [jnp.ones((2 * D_CONV,), f32), jnp.full((D_CONV,), BETA, f32)]
        + [jnp.concatenate([jnp.ones((2 * GROUP_WIDTH,), f32), jnp.full((GROUP_WIDTH,), BETA, f32)]) for _ in range(N_GROUPS)]
        + [jnp.ones((2 * D_MODEL,), f32)])
    return {
        'x_prompt': nrm(ks[0], (BATCH, SEQ, D_MODEL), 1.0),
        'x_sample': nrm(ks[1], (DEC_BATCH, DEC_SEQ, D_MODEL), 1.0),
        'state_conv': nrm(ks[2], (DEPTH, DEC_BATCH, CONV_W - 1, D_CONV), 1.0),
        'cache_kv_w128': kv_cache(ks[3], ATTN_GROUPS[0][0]),
        'cache_kv_w512': kv_cache(ks[4], ATTN_GROUPS[1][0]),
        'cache_kv_w2048': kv_cache(ks[5], ATTN_GROUPS[2][0]),
        'w_in': nrm(ks[6], (DEPTH, D_MODEL, N_IN), D_MODEL ** -0.5) * col_scale,
        'b_gate': nrm(ks[7], (DEPTH, 2 * D_MODEL), 0.02),
        'conv_w': nrm(ks[8], (DEPTH, CONV_W, D_CONV), CONV_W ** -0.5),
        'w_conv_out': nrm(ks[9], (DEPTH, D_CONV, D_MODEL), BETA * D_CONV ** -0.5),
        'w_attn_out': nrm(ks[10], (DEPTH, GROUP_WIDTH, D_MODEL), BETA * GROUP_WIDTH ** -0.5),
        'w_o': nrm(ks[11], (DEPTH, D_MODEL, D_MODEL), BETA * D_MODEL ** -0.5),
        'ln1_g': 1.0 + nrm(ks[12], (DEPTH, D_MODEL), 0.02),
        'ln1_b': nrm(ks[13], (DEPTH, D_MODEL), 0.02),
        'w_up': nrm(ks[14], (DEPTH, D_MODEL, D_FF), BETA * D_MODEL ** -0.5),
        'w_down': nrm(ks[15], (DEPTH, D_FF, D_MODEL), BETA * D_FF ** -0.5),
        'ln2_g': 1.0 + nrm(ks[16], (DEPTH, D_MODEL), 0.02),
        'ln2_b': nrm(ks[17], (DEPTH, D_MODEL), 0.02),
    }


def reference(x_prompt, x_sample, state_conv, cache_kv_w128, cache_kv_w512, cache_kv_w2048,
              w_in, b_gate, conv_w, w_conv_out, w_attn_out, w_o, ln1_g, ln1_b, w_up, w_down, ln2_g, ln2_b):
    n_p, s_p, _ = x_prompt.shape
    t_s = x_sample.shape[1]
    pos_p = jnp.arange(s_p, dtype=jnp.int32)
    pos_s = PAST_LEN + jnp.arange(t_s, dtype=jnp.int32)
    conv_zero = jnp.zeros((n_p, CONV_W - 1, D_CONV), x_prompt.dtype)
    hp, hs = x_prompt, x_sample
    cp_list, cs_list, kvp_list, kvs_list = [], [], [], []
    for l in range(DEPTH):
        weights = (w_in[l], b_gate[l], conv_w[l], w_conv_out[l], w_attn_out[l], w_o[l],
                   ln1_g[l], ln1_b[l], w_up[l], w_down[l], ln2_g[l], ln2_b[l])
        hp, cp, kvp = decoder_layer(hp, pos_p, conv_zero, None, *weights)
        hs, cs, kvs = decoder_layer(hs, pos_s, state_conv[l],
                                    (cache_kv_w128[l], cache_kv_w512[l], cache_kv_w2048[l]), *weights)
        cp_list.append(cp)
        cs_list.append(cs)
        kvp_list.append(kvp)
        kvs_list.append(kvs)
    conv_p = jnp.stack(cp_list, axis=0)
    conv_s = jnp.stack(cs_list, axis=0)
    kvp_g = [jnp.stack([kv[g] for kv in kvp_list], axis=0) for g in range(N_GROUPS)]
    kvs_g = [jnp.stack([kv[g] for kv in kvs_list], axis=0) for g in range(N_GROUPS)]
    return (hp, hs, conv_p, kvp_g[0], kvp_g[1], kvp_g[2], conv_s, kvs_g[0], kvs_g[1], kvs_g[2])
```

```python
import functools
import math

import jax
import jax.numpy as jnp
from jax import lax
from jax.experimental import pallas as pl
from jax.experimental.pallas import tpu as pltpu

F32 = jnp.float32
BF16 = jnp.bfloat16

HEAD_DIM = 64
HEADS = 4
GROUP_WIDTH = HEADS * HEAD_DIM
ATTN_GROUPS = ((128, 1), (512, 4), (2048, 16))
WIN_STEPS = 128
CONV_W = 3
ROPE_THETA = 10000.0
LN_EPS = 1e-5
ATTN_SCALE = HEAD_DIM ** -0.5

VMEM_LIMIT_BYTES = 56 * 1024 * 1024


def _dot(a, b):
    return jnp.dot(a, b, preferred_element_type=F32)


def _dot_nt(a, b):
    return lax.dot_general(a, b, (((1,), (1,)), ((), ())), preferred_element_type=F32)


def _head_of_lane(shape):
    return lax.broadcasted_iota(jnp.int32, shape, len(shape) - 1) // HEAD_DIM


def _layer_norm(x, g, b):
    mu = jnp.mean(x, axis=-1, keepdims=True)
    xc = x - mu
    var = jnp.mean(xc * xc, axis=-1, keepdims=True)
    return xc * lax.rsqrt(var + LN_EPS) * g + b


def _qkv_kernel(x_ref, w_ref, cos_ref, sin_ref, *out_refs, tm, tail_rows, for_prompt):
    xb = x_ref[0].astype(BF16)
    cos = cos_ref[...]
    sin = sin_ref[...]
    lane = lax.broadcasted_iota(jnp.int32, (1, GROUP_WIDTH), 1)
    first_half = (lane % HEAD_DIM) < (HEAD_DIM // 2)

    def rope(z):
        swapped = jnp.where(first_half,
                            pltpu.roll(z, GROUP_WIDTH - HEAD_DIM // 2, 1),
                            pltpu.roll(z, HEAD_DIM // 2, 1))
        return z * cos + swapped * sin

    per_group = 4 if for_prompt else 2
    for g in range(len(ATTN_GROUPS)):
        base = g * 3 * GROUP_WIDTH
        q = rope(_dot(xb, w_ref[:, base:base + GROUP_WIDTH])) * ATTN_SCALE
        k = rope(_dot(xb, w_ref[:, base + GROUP_WIDTH:base + 2 * GROUP_WIDTH]))
        v = _dot(xb, w_ref[:, base + 2 * GROUP_WIDTH:base + 3 * GROUP_WIDTH])
        refs = out_refs[g * per_group:(g + 1) * per_group]
        rows = tail_rows[g]
        if for_prompt:
            q_ref, k_ref, v_ref, kv_ref = refs
            q_ref[0] = q.astype(BF16)
            k_ref[0] = k.astype(BF16)
            v_ref[0] = v.astype(BF16)
        else:
            q_ref, kv_ref = refs
            q_ref[0] = q
        kv_ref[0, :, 0:GROUP_WIDTH] = k[tm - rows:tm]
        kv_ref[0, :, GROUP_WIDTH:2 * GROUP_WIDTH] = v[tm - rows:tm]


def _rope_tables(pos):
    inv = 1.0 / (ROPE_THETA ** (jnp.arange(0, HEAD_DIM, 2, dtype=F32) / HEAD_DIM))
    ang = pos.astype(F32)[:, None] * inv[None, :]
    cos = jnp.cos(ang)
    sin = jnp.sin(ang)
    cos = jnp.tile(jnp.concatenate([cos, cos], axis=-1), (1, HEADS))
    sin = jnp.tile(jnp.concatenate([-sin, sin], axis=-1), (1, HEADS))
    return cos, sin


def _qkv_call(x, w_qkv, cos, sin, *, tm, for_prompt):
    n, s, dm = x.shape
    tiles = s // tm
    tab_blocks = cos.shape[0] // tm
    n_groups = len(ATTN_GROUPS)

    out_shapes, out_specs, tail_rows = [], [], []
    for window, _ in ATTN_GROUPS:
        if for_prompt:
            w_rows = min(window, s)
            blk_rows = min(w_rows, tm)
            first_tail_tile = (s - w_rows) // tm
            for _ in range(3):
                out_shapes.append(jax.ShapeDtypeStruct((n, s, GROUP_WIDTH), BF16))
                out_specs.append(pl.BlockSpec((1, tm, GROUP_WIDTH), lambda i, j: (i, j, 0)))
            out_shapes.append(jax.ShapeDtypeStruct((n, w_rows, 2 * GROUP_WIDTH), F32))
            out_specs.append(pl.BlockSpec(
                (1, blk_rows, 2 * GROUP_WIDTH),
                lambda i, j, first=first_tail_tile: (i, jnp.maximum(j - first, 0), 0)))
            tail_rows.append(blk_rows)
        else:
            out_shapes.append(jax.ShapeDtypeStruct((n, s, GROUP_WIDTH), F32))
            out_specs.append(pl.BlockSpec((1, tm, GROUP_WIDTH), lambda i, j: (i, j, 0)))
            out_shapes.append(jax.ShapeDtypeStruct((n, s, 2 * GROUP_WIDTH), F32))
            out_specs.append(pl.BlockSpec((1, tm, 2 * GROUP_WIDTH), lambda i, j: (i, j, 0)))
            tail_rows.append(tm)

    kern = functools.partial(_qkv_kernel, tm=tm, tail_rows=tuple(tail_rows), for_prompt=for_prompt)
    return pl.pallas_call(
        kern,
        grid=(n, tiles),
        in_specs=[
            pl.BlockSpec((1, tm, dm), lambda i, j: (i, j, 0)),
            pl.BlockSpec(w_qkv.shape, lambda i, j: (0, 0), pipeline_mode=pl.Buffered(1)),
            pl.BlockSpec((tm, GROUP_WIDTH), lambda i, j: (j % tab_blocks, 0)),
            pl.BlockSpec((tm, GROUP_WIDTH), lambda i, j: (j % tab_blocks, 0)),
        ],
        out_specs=out_specs,
        out_shape=out_shapes,
        compiler_params=pltpu.CompilerParams(
            dimension_semantics=("arbitrary", "arbitrary"), vmem_limit_bytes=VMEM_LIMIT_BYTES),
        name="qkv_prompt" if for_prompt else "qkv_sample",
    )(x, w_qkv, cos, sin)


def _band_attention(q, kk, vv, min_col):
    blk = WIN_STEPS
    head = _head_of_lane((1, GROUP_WIDTH))
    qs = jnp.concatenate([jnp.where(head == h, q, jnp.zeros_like(q)) for h in range(HEADS)], axis=0)
    s = _dot_nt(qs, kk)
    row = lax.broadcasted_iota(jnp.int32, s.shape, 0) % blk
    col = lax.broadcasted_iota(jnp.int32, s.shape, 1)
    valid = (col >= row) & (col <= row + blk) & (col >= min_col)
    s = jnp.where(valid, s, -jnp.inf)
    m = jnp.max(s, axis=-1, keepdims=True)
    p = jnp.exp(s - m)
    l = jnp.sum(p, axis=-1, keepdims=True)
    o_all = _dot(p.astype(BF16), vv) / l
    lse = m + jnp.log(l)
    o = jnp.zeros((blk, GROUP_WIDTH), F32)
    lse_b = jnp.zeros((blk, GROUP_WIDTH), F32)
    for h in range(HEADS):
        sel = head == h
        o = o + jnp.where(sel, o_all[h * blk:(h + 1) * blk], 0.0)
        lse_b = lse_b + jnp.where(sel, lse[h * blk:(h + 1) * blk], 0.0)
    return o, lse_b


def _attn_prompt_kernel(*refs, tile):
    n_groups = len(ATTN_GROUPS)
    in_refs = refs[:5 * n_groups]
    out_refs = refs[5 * n_groups:7 * n_groups]
    scratch = refs[7 * n_groups:]
    j = pl.program_id(1)
    blk = WIN_STEPS
    for g, (_, dil) in enumerate(ATTN_GROUPS):
        q_ref, kc_ref, vc_ref, kp_ref, vp_ref = in_refs[5 * g:5 * g + 5]
        o_ref, l_ref = out_refs[2 * g:2 * g + 2]
        ks_ref, vs_ref = scratch[2 * g:2 * g + 2]
        rows = tile // dil
        nblk = rows // blk
        ks_ref[0:blk, :] = kp_ref[0]
        vs_ref[0:blk, :] = vp_ref[0]
        ks_ref[blk:blk + rows, :] = kc_ref[0]
        vs_ref[blk:blk + rows, :] = vc_ref[0]
        for r in range(dil):
            cols = slice(r * GROUP_WIDTH, (r + 1) * GROUP_WIDTH)

            def block(b, carry, cols=cols, q_ref=q_ref, ks_ref=ks_ref, vs_ref=vs_ref,
                      o_ref=o_ref, l_ref=l_ref, nblk=nblk):
                row0 = pl.multiple_of(b * blk, blk)
                q = q_ref[0, pl.ds(row0, blk), cols]
                kk = ks_ref[pl.ds(row0, 2 * blk), cols]
                vv = vs_ref[pl.ds(row0, 2 * blk), cols]
                min_col = jnp.where(j * nblk + b == 0, blk, 0)
                o, lse_b = _band_attention(q, kk, vv, min_col)
                o_ref[0, pl.ds(row0, blk), cols] = o
                l_ref[0, pl.ds(row0, blk), cols] = lse_b
                return carry

            if nblk == 1:
                block(0, 0)
            else:
                lax.fori_loop(0, nblk, block, 0)


def _attn_prompt_call(qkv, n, s):
    blk = WIN_STEPS
    tile = blk * max(d for _, d in ATTN_GROUPS)
    assert s % tile == 0
    args, in_specs, out_shapes, out_specs, scratch = [], [], [], [], []
    for (q, k, v), (_, dil) in zip(qkv, ATTN_GROUPS):
        rows, width = tile // dil, dil * GROUP_WIDTH
        view = (n, s // dil, width)
        nblk = rows // blk
        cur = pl.BlockSpec((1, rows, width), lambda i, j: (i, j, 0))
        prev = pl.BlockSpec((1, blk, width), lambda i, j, nblk=nblk: (i, jnp.maximum(j * nblk - 1, 0), 0))
        kv, vv = k.reshape(view), v.reshape(view)
        args += [q.reshape(view), kv, vv, kv, vv]
        in_specs += [cur, cur, cur, prev, prev]
        out_shapes += [jax.ShapeDtypeStruct(view, F32)] * 2
        out_specs += [cur, cur]
        scratch += [pltpu.VMEM((blk + rows, width), BF16)] * 2
    outs = pl.pallas_call(
        functools.partial(_attn_prompt_kernel, tile=tile),
        grid=(n, s // tile),
        in_specs=in_specs,
        out_specs=out_specs,
        out_shape=out_shapes,
        scratch_shapes=scratch,
        compiler_params=pltpu.CompilerParams(
            dimension_semantics=("arbitrary", "arbitrary"), vmem_limit_bytes=VMEM_LIMIT_BYTES),
        name="attn_prompt",
    )(*args)
    return [(outs[2 * g].reshape(n * s, GROUP_WIDTH), outs[2 * g + 1].reshape(n * s, GROUP_WIDTH))
            for g in range(len(ATTN_GROUPS))]


KEY_PAD = 128


def _attn_sample_kernel(*refs, t_len):
    n_groups = len(ATTN_GROUPS)
    in_refs = refs[:3 * n_groups]
    out_refs = refs[3 * n_groups:]
    head = _head_of_lane((1, GROUP_WIDTH))
    for g, (window, dil) in enumerate(ATTN_GROUPS):
        q_ref, kvn_ref, c_ref = in_refs[3 * g:3 * g + 3]
        co_ref, o_ref, l_ref = out_refs[3 * g:3 * g + 3]
        w_rows = c_ref.shape[1]
        co_ref[0, 0:w_rows - t_len, :] = c_ref[0, t_len:w_rows, :]
        kvn = kvn_ref[0]
        co_ref[0, w_rows - t_len:w_rows, :] = kvn

        pad = jnp.zeros((KEY_PAD - t_len, GROUP_WIDTH), F32)
        k_ext = jnp.concatenate([c_ref[0, :, 0:GROUP_WIDTH], kvn[:, 0:GROUP_WIDTH], pad], axis=0).astype(BF16)
        v_ext = jnp.concatenate(
            [c_ref[0, :, GROUP_WIDTH:2 * GROUP_WIDTH], kvn[:, GROUP_WIDTH:2 * GROUP_WIDTH], pad], axis=0).astype(BF16)
        q = q_ref[0]
        qs = jnp.concatenate([jnp.where(head == h, q, 0.0) for h in range(HEADS)], axis=0).astype(BF16)
        s = _dot_nt(qs, k_ext)
        t = lax.broadcasted_iota(jnp.int32, s.shape, 0) % t_len
        col = lax.broadcasted_iota(jnp.int32, s.shape, 1)
        t_new = col - w_rows
        valid_cache = (col < w_rows) & (col >= t) & (((col - t) & (dil - 1)) == 0)
        valid_new = (t_new >= 0) & (t_new <= t) & (((t - t_new) & (dil - 1)) == 0)
        s = jnp.where(valid_cache | valid_new, s, -jnp.inf)
        m = jnp.max(s, axis=-1, keepdims=True)
        p = jnp.exp(s - m)
        l = jnp.sum(p, axis=-1, keepdims=True)
        o_all = _dot(p.astype(BF16), v_ext) / l
        lse = m + jnp.log(l)
        o = jnp.zeros((t_len, GROUP_WIDTH), F32)
        lse_b = jnp.zeros((t_len, GROUP_WIDTH), F32)
        for h in range(HEADS):
            sel = head == h
            o = o + jnp.where(sel, o_all[h * t_len:(h + 1) * t_len], 0.0)
            lse_b = lse_b + jnp.where(sel, lse[h * t_len:(h + 1) * t_len], 0.0)
        o_ref[0] = o
        l_ref[0] = lse_b


def _attn_sample_call(q_kvn, caches, n, t_len):
    args, in_specs, out_shapes, out_specs = [], [], [], []
    for (q, kvn), cache, (window, dil) in zip(q_kvn, caches, ATTN_GROUPS):
        w_rows = cache.shape[1]
        assert w_rows == WIN_STEPS * dil and dil & (dil - 1) == 0
        small = pl.BlockSpec((1, t_len, GROUP_WIDTH), lambda i: (i, 0, 0))
        new = pl.BlockSpec((1, t_len, 2 * GROUP_WIDTH), lambda i: (i, 0, 0))
        big = pl.BlockSpec((1, w_rows, 2 * GROUP_WIDTH), lambda i: (i, 0, 0))
        args += [q, kvn, cache]
        in_specs += [small, new, big]
        out_shapes += [jax.ShapeDtypeStruct(cache.shape, F32),
                       jax.ShapeDtypeStruct((n, t_len, GROUP_WIDTH), F32),
                       jax.ShapeDtypeStruct((n, t_len, GROUP_WIDTH), F32)]
        out_specs += [big, small, small]
    outs = pl.pallas_call(
        functools.partial(_attn_sample_kernel, t_len=t_len),
        grid=(n,),
        in_specs=in_specs,
        out_specs=out_specs,
        out_shape=out_shapes,
        compiler_params=pltpu.CompilerParams(
            dimension_semantics=("arbitrary",), vmem_limit_bytes=VMEM_LIMIT_BYTES),
        name="attn_sample",
    )(*args)
    new_caches = [outs[3 * g] for g in range(len(ATTN_GROUPS))]
    o_lse = [(outs[3 * g + 1].reshape(n * t_len, GROUP_WIDTH), outs[3 * g + 2].reshape(n * t_len, GROUP_WIDTH))
             for g in range(len(ATTN_GROUPS))]
    return new_caches, o_lse


CARRY_ROWS = 8


def _mix_kernel(*refs, tm, seq_tiles, seg_len, alpha, d_conv, d_model, d_ff, ff_chunk):
    n_groups = len(ATTN_GROUPS)
    x_ref = refs[0]
    ol_refs = refs[1:1 + 2 * n_groups]
    rest = refs[1 + 2 * n_groups:]
    if seg_len:
        e1_ref, e2_ref = rest[:2]
        rest = rest[2:]
    (wc_ref, wg_ref, bg_ref, cw_ref, wco_ref, wao_ref, wo_ref, g1_ref, b1_ref,
     wup_ref, wdn_ref, g2_ref, b2_ref, y_ref, u_ref) = rest[:15]
    carry_ref = rest[15] if seq_tiles else None

    x = x_ref[...]
    xb = x.astype(BF16)

    gb = _dot(xb, wc_ref[:, 0:d_conv])
    u = _dot(xb, wc_ref[:, d_conv:2 * d_conv]) * _dot(xb, wc_ref[:, 2 * d_conv:3 * d_conv])
    row = lax.broadcasted_iota(jnp.int32, (tm, 1), 0)
    u1 = pltpu.roll(u, 1, 0)
    u2 = pltpu.roll(u, 2, 0)
    if seq_tiles:
        @pl.when(pl.program_id(0) % seq_tiles == 0)
        def _():
            carry_ref[...] = jnp.zeros_like(carry_ref)
        prev0 = carry_ref[CARRY_ROWS - 2:CARRY_ROWS - 1, :]
        prev1 = carry_ref[CARRY_ROWS - 1:CARRY_ROWS, :]
        u1 = jnp.where(row == 0, prev1, u1)
        u2 = jnp.where(row == 0, prev0, jnp.where(row == 1, prev1, u2))
        carry_ref[...] = u[tm - CARRY_ROWS:tm]
        u_ref[0] = u[tm - CARRY_ROWS:tm]
    else:
        t = row % seg_len
        u1 = jnp.where(t == 0, e1_ref[...], u1)
        u2 = jnp.where(t < 2, e2_ref[...], u2)
        u_ref[...] = u
    yc = gb * (cw_ref[0:1, :] * u2 + cw_ref[1:2, :] * u1 + cw_ref[2:3, :] * u)

    lses = [ol_refs[2 * g + 1][...] for g in range(n_groups)]
    mx = functools.reduce(jnp.maximum, lses)
    es = [jnp.exp(l - mx) for l in lses]
    den = functools.reduce(lambda a, b: a + b, es)
    ya = functools.reduce(lambda a, b: a + b, [(e / den) * ol_refs[2 * g][...] for g, e in enumerate(es)])

    gates = jax.nn.sigmoid(_dot(xb, wg_ref[...]) + bg_ref[...])
    mixed = (gates[:, 0:d_model] * _dot(yc.astype(BF16), wco_ref[...])
             + gates[:, d_model:2 * d_model] * _dot(ya.astype(BF16), wao_ref[...]))
    x1 = _layer_norm(alpha * x + _dot(mixed.astype(BF16), wo_ref[...]), g1_ref[...], b1_ref[...])

    x1b = x1.astype(BF16)
    acc = jnp.zeros((tm, d_model), F32)
    for c in range(d_ff // ff_chunk):
        cs = slice(c * ff_chunk, (c + 1) * ff_chunk)
        hid = jnp.square(jnp.maximum(_dot(x1b, wup_ref[:, cs]), 0.0))
        acc = acc + _dot(hid.astype(BF16), wdn_ref[cs, :])
    y_ref[...] = _layer_norm(alpha * x1 + acc, g2_ref[...], b2_ref[...])


def _mix_call(x, o_lse, weights, *, tm, alpha, seq_len=0, seg_len=0, conv_hist=None):
    rows, d_model = x.shape
    (wc, wg, bg, cw, wco, wao, wo, g1, b1, wup, wdn, g2, b2) = weights
    d_conv = cw.shape[1]
    d_ff = wup.shape[1]
    seq_tiles = seq_len // tm if seq_len else 0

    def tile_spec(width):
        return pl.BlockSpec((tm, width), lambda i: (i, 0))

    def resident(a):
        return pl.BlockSpec(a.shape, lambda i: (0,) * a.ndim, pipeline_mode=pl.Buffered(1))

    args = [x]
    in_specs = [tile_spec(d_model)]
    for o, lse in o_lse:
        args += [o, lse]
        in_specs += [tile_spec(GROUP_WIDTH)] * 2
    if seg_len:
        args += list(conv_hist)
        in_specs += [tile_spec(d_conv)] * 2
    args += list(weights)
    in_specs += [resident(a) for a in weights]

    if seq_tiles:
        n_seq = rows // seq_len
        u_shape = jax.ShapeDtypeStruct((n_seq, CARRY_ROWS, d_conv), F32)
        u_spec = pl.BlockSpec((1, CARRY_ROWS, d_conv), lambda i: (i // seq_tiles, 0, 0))
        scratch = [pltpu.VMEM((CARRY_ROWS, d_conv), F32)]
    else:
        u_shape = jax.ShapeDtypeStruct((rows, d_conv), F32)
        u_spec = tile_spec(d_conv)
        scratch = []

    kern = functools.partial(_mix_kernel, tm=tm, seq_tiles=seq_tiles, seg_len=seg_len, alpha=alpha,
                             d_conv=d_conv, d_model=d_model, d_ff=d_ff, ff_chunk=min(d_ff, 1024))
    return pl.pallas_call(
        kern,
        grid=(rows // tm,),
        in_specs=in_specs,
        out_specs=[tile_spec(d_model), u_spec],
        out_shape=[jax.ShapeDtypeStruct((rows, d_model), F32), u_shape],
        scratch_shapes=scratch,
        compiler_params=pltpu.CompilerParams(
            dimension_semantics=("arbitrary",), vmem_limit_bytes=VMEM_LIMIT_BYTES),
        name="mix_prompt" if seq_tiles else "mix_sample",
    )(*args)


QKV_TILE = 512
MIX_TILE = 256


def kernel(x_prompt, x_sample, state_conv, cache_kv_w128, cache_kv_w512, cache_kv_w2048, w_in, b_gate, conv_w,
           w_conv_out, w_attn_out, w_o, ln1_g, ln1_b, w_up, w_down, ln2_g, ln2_b):
    depth = w_in.shape[0]
    assert depth == 1, "single-layer trunk"
    n_p, s_p, d_model = x_prompt.shape
    n_s, t_s, _ = x_sample.shape
    d_conv = conv_w.shape[2]
    n_groups = len(ATTN_GROUPS)
    past_len = cache_kv_w2048.shape[2]
    alpha = (2.0 * depth) ** 0.25

    w = w_in[0]
    qkv_lo, qkv_hi = 3 * d_conv, 3 * d_conv + 3 * n_groups * GROUP_WIDTH
    w_c = w[:, :qkv_lo].astype(BF16)
    w_qkv = w[:, qkv_lo:qkv_hi].astype(BF16)
    w_g = w[:, qkv_hi:].astype(BF16)
    row2d = lambda a: a.reshape(1, -1)
    mix_weights = (w_c, w_g, row2d(b_gate[0]), conv_w[0], w_conv_out[0].astype(BF16), w_attn_out[0].astype(BF16),
                   w_o[0].astype(BF16), row2d(ln1_g[0]), row2d(ln1_b[0]), w_up[0].astype(BF16),
                   w_down[0].astype(BF16), row2d(ln2_g[0]), row2d(ln2_b[0]))

    cos_p, sin_p = _rope_tables(jnp.arange(s_p, dtype=jnp.int32))
    outs = _qkv_call(x_prompt, w_qkv, cos_p, sin_p, tm=QKV_TILE, for_prompt=True)
    qkv_p = [tuple(outs[4 * g:4 * g + 3]) for g in range(n_groups)]
    kv_p = [outs[4 * g + 3] for g in range(n_groups)]
    o_lse_p = _attn_prompt_call(qkv_p, n_p, s_p)
    y_p, u_tail_p = _mix_call(x_prompt.reshape(n_p * s_p, d_model), o_lse_p, mix_weights,
                              tm=MIX_TILE, alpha=alpha, seq_len=s_p)
    conv_p = u_tail_p[:, CARRY_ROWS - (CONV_W - 1):, :][None]

    rows_s = n_s * t_s
    pos_s = past_len + (jnp.arange(QKV_TILE, dtype=jnp.int32) % t_s)
    cos_s, sin_s = _rope_tables(pos_s)
    outs = _qkv_call(x_sample.reshape(1, rows_s, d_model), w_qkv, cos_s, sin_s, tm=QKV_TILE, for_prompt=False)
    q_kvn = [(outs[2 * g].reshape(n_s, t_s, GROUP_WIDTH), outs[2 * g + 1].reshape(n_s, t_s, 2 * GROUP_WIDTH))
             for g in range(n_groups)]
    caches = [c[0].reshape(n_s, c.shape[2], 2 * GROUP_WIDTH) for c in (cache_kv_w128, cache_kv_w512, cache_kv_w2048)]
    new_caches, o_lse_s = _attn_sample_call(q_kvn, caches, n_s, t_s)
    prev = state_conv[0]
    zeros = jnp.zeros((n_s, t_s - 2, d_conv), F32)
    e2 = jnp.concatenate([prev, zeros], axis=1).reshape(rows_s, d_conv)
    e1 = jnp.concatenate([prev[:, 1:], jnp.zeros((n_s, t_s - 1, d_conv), F32)], axis=1).reshape(rows_s, d_conv)
    y_s, u_s = _mix_call(x_sample.reshape(rows_s, d_model), o_lse_s, mix_weights,
                         tm=MIX_TILE, alpha=alpha, seg_len=t_s, conv_hist=(e1, e2))
    conv_s = u_s.reshape(n_s, t_s, d_conv)[:, t_s - (CONV_W - 1):, :][None]

    def kv6(a):
        return a.reshape(1, a.shape[0], a.shape[1], 2, HEADS, HEAD_DIM)

    return (y_p.reshape(n_p, s_p, d_model), y_s.reshape(n_s, t_s, d_model), conv_p,
            kv6(kv_p[0]), kv6(kv_p[1]), kv6(kv_p[2]), conv_s,
            kv6(new_caches[0]), kv6(new_caches[1]), kv6(new_caches[2]))
```
